```python
import math
import jax, jax.numpy as jnp
from jax import lax
import numpy as np

D_MODEL = 1024
BATCH = 16
SEQ = 256
DEPTH = 4
DEC_BATCH = 4
DEC_SEQ = 1024
PAST_LEN = 512

GRID_W = 64
N_MIXERS = 3
N_DN = (DEPTH + 2) // 3
N_SC = (DEPTH + 1) // 3
N_NA = DEPTH // 3
N_DENSE = (DEPTH + 1) // 2
N_MOE = DEPTH // 2

DN_DK = 128
DN_DV = 128
DN_HEADS = D_MODEL // DN_DK
DN_CONV = 3
DN_CHUNK = 64

SC_CONV = 3

NA_HD = 64
NA_HEADS = D_MODEL // NA_HD
WIN_H = 8
WIN_W = 16
NA_QB = WIN_W
NA_KB = 2 * WIN_W

D_FF = 2816
N_EXPERTS = 8
TOP_K = 2
D_FF_EXPERT = 1408

EPS = 1e-6
NEG_INF = -1e30

kernel_name = "hybrid_diffusion_trunk_step"


def rms_norm(x, g):
    x32 = x.astype(jnp.float32)
    y = x32 * lax.rsqrt(jnp.mean(x32 * x32, axis=-1, keepdims=True) + EPS)
    return (y * g.astype(jnp.float32)).astype(x.dtype)


def l2_norm(x):
    x32 = x.astype(jnp.float32)
    return x32 * lax.rsqrt(jnp.sum(x32 * x32, axis=-1, keepdims=True) + EPS)


def modulate(x, g, shift, scale):
    return rms_norm(x, g) * (1 + scale) + shift


def conv_centred(x, w):
    k = w.shape[0]
    t = x.shape[1]
    p = k // 2
    xp = jnp.pad(x, ((0, 0), (p, p), (0, 0)))
    return sum(xp[:, j:j + t] * w[j] for j in range(k))


def chunk_gated_delta(q, k, v, g, beta, s0):
    b, h, t, dk = q.shape
    dv = v.shape[-1]
    c = DN_CHUNK
    n = t // c
    q = q.reshape(b, h, n, c, dk)
    k = k.reshape(b, h, n, c, dk)
    v = v.reshape(b, h, n, c, dv)
    g = jnp.cumsum(g.reshape(b, h, n, c), axis=-1)
    beta = beta.reshape(b, h, n, c)
    incl = jnp.tril(jnp.ones((c, c), dtype=bool))
    strict = jnp.tril(jnp.ones((c, c), dtype=bool), -1)
    diff = g[..., :, None] - g[..., None, :]
    decay = jnp.where(incl, jnp.exp(jnp.where(incl, diff, 0.0)), 0.0)
    kb = k * beta[..., None]
    a = jnp.where(strict, jnp.einsum('bhnid,bhnjd->bhnij', kb, k) * decay, 0.0)
    lower = a + jnp.eye(c, dtype=a.dtype)
    rhs = jnp.concatenate([v * beta[..., None], kb * jnp.exp(g)[..., None]], axis=-1)
    sol = lax.linalg.triangular_solve(lower, rhs, left_side=True, lower=True, unit_diagonal=True)
    u, w = sol[..., :dv], sol[..., dv:]
    qk = jnp.einsum('bhnid,bhnjd->bhnij', q, k) * decay

    def step(s, xs):
        q_c, k_c, u_c, w_c, g_c, qk_c = xs
        v_new = u_c - jnp.einsum('bhck,bhkv->bhcv', w_c, s)
        o_c = (jnp.einsum('bhck,bhkv->bhcv', q_c * jnp.exp(g_c)[..., None], s)
               + jnp.einsum('bhij,bhjv->bhiv', qk_c, v_new))
        g_last = g_c[..., -1:]
        s = (s * jnp.exp(g_last)[..., None]
             + jnp.einsum('bhck,bhcv->bhkv', k_c * jnp.exp(g_last - g_c)[..., None], v_new))
        return s, o_c

    xs = tuple(jnp.moveaxis(z, 2, 0) for z in (q, k, u, w, g, qk))
    s_final, o = lax.scan(step, s0, xs)
    o = jnp.moveaxis(o, 0, 2).reshape(b, h, t, dv)
    return o, s_final


def deltanet_mixer(h, s0, w_in, w_ab, conv_w, a_log, dt_bias, norm_g, w_out):
    b, t, _ = h.shape
    proj = h @ w_in
    qkv = jax.nn.silu(conv_centred(proj[..., :3 * D_MODEL], conv_w))
    z = proj[..., 3 * D_MODEL:]
    q, k, v = jnp.split(qkv, 3, axis=-1)

    def heads(x, d):
        return x.reshape(b, t, DN_HEADS, d).transpose(0, 2, 1, 3).astype(jnp.float32)

    q = l2_norm(heads(q, DN_DK)) * (DN_DK ** -0.5)
    k = l2_norm(heads(k, DN_DK))
    v = heads(v, DN_DV)
    ab = (h @ w_ab).astype(jnp.float32).reshape(b, t, 2, 2, DN_HEADS)
    g = -jnp.exp(a_log.astype(jnp.float32)) * jax.nn.softplus(ab[:, :, :, 0] + dt_bias.astype(jnp.float32))
    beta = jax.nn.sigmoid(ab[:, :, :, 1])
    g = g.transpose(0, 2, 3, 1)
    beta = beta.transpose(0, 2, 3, 1)
    s0 = s0.astype(jnp.float32)
    o_f, s_f = chunk_gated_delta(q, k, v, g[:, 0], beta[:, 0], s0[:, 0])

    def flip(x):
        return jnp.flip(x, axis=2)

    o_b, s_b = chunk_gated_delta(flip(q), flip(k), flip(v), flip(g[:, 1]), flip(beta[:, 1]), s0[:, 1])
    o = (o_f + flip(o_b)).transpose(0, 2, 1, 3)
    o = rms_norm(o, norm_g) * jax.nn.silu(z.reshape(b, t, DN_HEADS, DN_DV).astype(jnp.float32))
    out = o.reshape(b, t, D_MODEL).astype(h.dtype) @ w_out
    return out, jnp.stack([s_f, s_b], axis=1)


def shortconv_mixer(h, w_in, conv_w, w_out):
    bg, cg, u = jnp.split(h @ w_in, 3, axis=-1)
    return (bg * conv_centred(cg * u, conv_w)) @ w_out


def na_project(h, w_qkv, q_norm, k_norm):
    b, t, _ = h.shape
    q, k, v = jnp.split(h @ w_qkv, 3, axis=-1)

    def heads(x):
        return x.reshape(b, t, NA_HEADS, NA_HD).transpose(0, 2, 1, 3)

    return rms_norm(heads(q), q_norm), rms_norm(heads(k), k_norm), heads(v)


def merge_heads(o):
    b, h, t, d = o.shape
    return o.transpose(0, 2, 1, 3).reshape(b, t, h * d)


def na_context(h, w_qkv, q_norm, k_norm, w_out):
    q, k, v = na_project(h, w_qkv, q_norm, k_norm)
    s = jnp.einsum('bhqd,bhkd->bhqk', q, k).astype(jnp.float32) * (NA_HD ** -0.5)
    p = jax.nn.softmax(s, axis=-1).astype(v.dtype)
    o = jnp.einsum('bhqk,bhkd->bhqd', p, v)
    return merge_heads(o).astype(h.dtype) @ w_out, k, v


def na_latent(h, k_ctx, v_ctx, w_qkv, q_norm, k_norm, rpb, w_out):
    b, n, _ = h.shape
    rows = n // GRID_W
    kh = min(WIN_H, rows)
    ncb = GRID_W // NA_QB
    q, k, v = na_project(h, w_qkv, q_norm, k_norm)
    r = jnp.arange(rows)
    key_rows = jnp.clip(r - kh // 2, 0, rows - kh)[:, None] + jnp.arange(kh)
    c0 = jnp.arange(ncb) * NA_QB
    key_cols = jnp.clip(c0 - WIN_W // 2, 0, GRID_W - NA_KB)[:, None] + jnp.arange(NA_KB)
    q_cols = c0[:, None] + jnp.arange(NA_QB)
    col_start = jnp.clip(q_cols - WIN_W // 2, 0, GRID_W - WIN_W)[..., None]
    col_ok = (key_cols[:, None, :] >= col_start) & (key_cols[:, None, :] < col_start + WIN_W)
    mask = jnp.broadcast_to(col_ok[:, :, None, :], (ncb, NA_QB, kh, NA_KB)).reshape(ncb, NA_QB, kh * NA_KB)
    idx = (key_rows[:, None, :, None] * GRID_W + key_cols[None, :, None, :]).reshape(rows, ncb, kh * NA_KB)
    k_blk = jnp.take(k, idx, axis=2)
    v_blk = jnp.take(v, idx, axis=2)
    q_blk = q.reshape(b, NA_HEADS, rows, ncb, NA_QB, NA_HD)
    dr = key_rows - r[:, None] + (WIN_H - 1)
    dc = jnp.clip(key_cols[:, None, :] - q_cols[:, :, None] + (WIN_W - 1), 0, 2 * WIN_W - 2)
    bias = rpb[:, dr[:, None, None, :, None], dc[None, :, :, None, :]]
    bias = bias.reshape(NA_HEADS, rows, ncb, NA_QB, kh * NA_KB).astype(jnp.float32)
    scale = NA_HD ** -0.5
    s_loc = jnp.einsum('bhrcqd,bhrckd->bhrcqk', q_blk, k_blk).astype(jnp.float32) * scale + bias
    s_loc = jnp.where(mask, s_loc, NEG_INF)
    s_ctx = jnp.einsum('bhrcqd,bhld->bhrcql', q_blk, k_ctx).astype(jnp.float32) * scale
    p = jax.nn.softmax(jnp.concatenate([s_loc, s_ctx], axis=-1), axis=-1)
    nloc = kh * NA_KB
    o = (jnp.einsum('bhrcqk,bhrckd->bhrcqd', p[..., :nloc].astype(v.dtype), v_blk)
         + jnp.einsum('bhrcql,bhld->bhrcqd', p[..., nloc:].astype(v_ctx.dtype), v_ctx))
    o = o.reshape(b, NA_HEADS, n, NA_HD)
    return merge_heads(o).astype(h.dtype) @ w_out


def dense_ffn(h, w_gu, w_down):
    g, u = jnp.split(h @ w_gu, 2, axis=-1)
    return (jax.nn.silu(g) * u) @ w_down


def moe_ffn(h, router, w_gu, w_down):
    logits = (h @ router).astype(jnp.float32)
    top_v, top_i = lax.top_k(logits, TOP_K)
    w = jax.nn.softmax(top_v, axis=-1)
    combine = jnp.sum(jax.nn.one_hot(top_i, N_EXPERTS, dtype=jnp.float32) * w[..., None], axis=-2)
    gu = jnp.einsum('btd,edf->btef', h, w_gu)
    g, u = jnp.split(gu, 2, axis=-1)
    act = jax.nn.silu(g) * u * combine[..., None].astype(h.dtype)
    return jnp.einsum('btef,efd->btd', act, w_down)


def setup_inputs(seed: int = 0) -> dict:
    key = jax.random.key(seed)
    ks = iter(jax.random.split(key, 48))
    f32 = jnp.float32
    D = D_MODEL

    def nrm(shape, s):
        return jax.random.normal(next(ks), shape, f32) * s

    def gain(shape):
        return 1.0 + nrm(shape, 0.05)

    x_prompt = nrm((BATCH, SEQ, D), 1.0)
    x_sample = nrm((DEC_BATCH, DEC_SEQ, D), 1.0)
    state_dn = nrm((DEC_BATCH, N_DN, 2, DN_HEADS, DN_DK, DN_DV), 0.1)
    cache_na_k = nrm((DEC_BATCH, N_NA, NA_HEADS, PAST_LEN, NA_HD), 1.0)
    cache_na_v = nrm((DEC_BATCH, N_NA, NA_HEADS, PAST_LEN, NA_HD), 1.0)
    c = nrm((DEC_BATCH, D), 1.0)
    c_ctx = nrm((D,), 1.0)
    ada_w = nrm((DEPTH, D, 6 * D), D ** -0.5)
    ada_b = nrm((DEPTH, 6 * D), 0.02)
    norm_mix = gain((DEPTH, D))
    norm_ffn = gain((DEPTH, D))
    dn_w_in = nrm((N_DN, D, 4 * D), D ** -0.5)
    dn_w_ab = nrm((N_DN, D, 4 * DN_HEADS), D ** -0.5)
    dn_conv = nrm((N_DN, DN_CONV, 3 * D), DN_CONV ** -0.5)
    dn_a_log = jnp.log(jax.random.uniform(next(ks), (N_DN, 2, DN_HEADS), f32, 1.0, 16.0))
    dt = jnp.exp(jax.random.uniform(next(ks), (N_DN, 2, DN_HEADS), f32, math.log(1e-3), math.log(1e-1)))
    dn_dt_bias = dt + jnp.log(-jnp.expm1(-dt))
    dn_norm = gain((N_DN, DN_DV))
    dn_w_out = nrm((N_DN, D, D), D ** -0.5)
    sc_w_in = nrm((N_SC, D, 3 * D), D ** -0.5)
    sc_conv = nrm((N_SC, SC_CONV, D), SC_CONV ** -0.5)
    sc_w_out = nrm((N_SC, D, D), D ** -0.5)
    na_w_qkv = nrm((N_NA, D, 3 * D), D ** -0.5)
    na_q_norm = gain((N_NA, NA_HD))
    na_k_norm = gain((N_NA, NA_HD))
    na_rpb = nrm((N_NA, NA_HEADS, 2 * WIN_H - 1, 2 * WIN_W - 1), 0.1)
    na_w_out = nrm((N_NA, D, D), D ** -0.5)
    ff_w_gu = nrm((N_DENSE, D, 2 * D_FF), D ** -0.5)
    ff_w_down = nrm((N_DENSE, D_FF, D), D_FF ** -0.5)
    moe_router = nrm((N_MOE, D, N_EXPERTS), D ** -0.5)
    moe_w_gu = nrm((N_MOE, N_EXPERTS, D, 2 * D_FF_EXPERT), D ** -0.5)
    moe_w_down = nrm((N_MOE, N_EXPERTS, D_FF_EXPERT, D), D_FF_EXPERT ** -0.5)
    return {
        "x_prompt": x_prompt, "x_sample": x_sample,
        "state_dn": state_dn, "cache_na_k": cache_na_k, "cache_na_v": cache_na_v,
        "c": c, "c_ctx": c_ctx,
        "ada_w": ada_w, "ada_b": ada_b, "norm_mix": norm_mix, "norm_ffn": norm_ffn,
        "dn_w_in": dn_w_in, "dn_w_ab": dn_w_ab, "dn_conv": dn_conv, "dn_a_log": dn_a_log,
        "dn_dt_bias": dn_dt_bias, "dn_norm": dn_norm, "dn_w_out": dn_w_out,
        "sc_w_in": sc_w_in, "sc_conv": sc_conv, "sc_w_out": sc_w_out,
        "na_w_qkv": na_w_qkv, "na_q_norm": na_q_norm, "na_k_norm": na_k_norm,
        "na_rpb": na_rpb, "na_w_out": na_w_out,
        "ff_w_gu": ff_w_gu, "ff_w_down": ff_w_down,
        "moe_router": moe_router, "moe_w_gu": moe_w_gu, "moe_w_down": moe_w_down,
    }


def reference(x_prompt, x_sample, state_dn, cache_na_k, cache_na_v, c, c_ctx,
              ada_w, ada_b, norm_mix, norm_ffn,
              dn_w_in, dn_w_ab, dn_conv, dn_a_log, dn_dt_bias, dn_norm, dn_w_out,
              sc_w_in, sc_conv, sc_w_out,
              na_w_qkv, na_q_norm, na_k_norm, na_rpb, na_w_out,
              ff_w_gu, ff_w_down, moe_router, moe_w_gu, moe_w_down):
    xp = x_prompt
    xs = x_sample
    new_dn, new_k, new_v = [], [], []
    for i in range(DEPTH):
        m_ctx = jnp.split(jax.nn.silu(c_ctx) @ ada_w[i] + ada_b[i], 6, axis=-1)
        m_lat = [z[:, None, :] for z in jnp.split(jax.nn.silu(c) @ ada_w[i] + ada_b[i], 6, axis=-1)]
        hp = modulate(xp, norm_mix[i], m_ctx[0], m_ctx[1])
        hs = modulate(xs, norm_mix[i], m_lat[0], m_lat[1])
        kind = i % N_MIXERS
        j = i // N_MIXERS
        if kind == 0:
            wts = (dn_w_in[j], dn_w_ab[j], dn_conv[j], dn_a_log[j], dn_dt_bias[j], dn_norm[j], dn_w_out[j])
            s_zero = jnp.zeros((xp.shape[0], 2, DN_HEADS, DN_DK, DN_DV), jnp.float32)
            yp, st = deltanet_mixer(hp, s_zero, *wts)
            ys, _ = deltanet_mixer(hs, state_dn[:, j], *wts)
            new_dn.append(st)
        elif kind == 1:
            yp = shortconv_mixer(hp, sc_w_in[j], sc_conv[j], sc_w_out[j])
            ys = shortconv_mixer(hs, sc_w_in[j], sc_conv[j], sc_w_out[j])
        else:
            yp, kc, vc = na_context(hp, na_w_qkv[j], na_q_norm[j], na_k_norm[j], na_w_out[j])
            ys = na_latent(hs, cache_na_k[:, j], cache_na_v[:, j], na_w_qkv[j], na_q_norm[j],
                           na_k_norm[j], na_rpb[j], na_w_out[j])
            new_k.append(kc)
            new_v.append(vc)
        xp = xp + m_ctx[2] * yp
        xs = xs + m_lat[2] * ys
        hp = modulate(xp, norm_ffn[i], m_ctx[3], m_ctx[4])
        hs = modulate(xs, norm_ffn[i], m_lat[3], m_lat[4])
        f = i // 2
        if i % 2 == 0:
            fp = dense_ffn(hp, ff_w_gu[f], ff_w_down[f])
            fs = dense_ffn(hs, ff_w_gu[f], ff_w_down[f])
        else:
            fp = moe_ffn(hp, moe_router[f], moe_w_gu[f], moe_w_down[f])
            fs = moe_ffn(hs, moe_router[f], moe_w_gu[f], moe_w_down[f])
        xp = xp + m_ctx[5] * fp
        xs = xs + m_lat[5] * fs
    return (xp, xs, jnp.stack(new_dn, axis=1), jnp.stack(new_k, axis=1), jnp.stack(new_v, axis=1))
```

```python
import functools
import math

import jax
import jax.numpy as jnp
from jax import lax
from jax.experimental import pallas as pl
from jax.experimental.pallas import tpu as pltpu

D = 1024
N_CTX_SEQ, CTX_T = 16, 256
N_LAT_SEQ, LAT_T = 4, 1024
M_CTX = N_CTX_SEQ * CTX_T
M_LAT = N_LAT_SEQ * LAT_T
M_ALL = M_CTX + M_LAT
DEPTH = 4
PAST = 512
GRID_W = 64
DN_H, DN_DK = 8, 128
NA_H, NA_HD = 16, 64
WIN_H, WIN_W = 8, 16
D_FF = 2816
N_EXP = 8
D_FFE = 1408
EPS = 1e-6
NEG_INF = -1e30

BF = jnp.bfloat16
F32 = jnp.float32

VMEM_LIMIT = 56 * 1024 * 1024


def _cp(*sem):
    return pltpu.CompilerParams(dimension_semantics=sem, vmem_limit_bytes=VMEM_LIMIT)


def _dot(a, b):
    return jnp.dot(a, b, preferred_element_type=F32)


def _dot_nt(a, b):
    return lax.dot_general(a, b, (((1,), (1,)), ((), ())), preferred_element_type=F32)


def _split3(x):
    hi = x.astype(BF)
    r = x - hi.astype(F32)
    mid = r.astype(BF)
    lo = (r - mid.astype(F32)).astype(BF)
    return hi, mid, lo


def _sigmoid(x):
    return 1.0 / (1.0 + jnp.exp(-x))


def _silu(x):
    return x * _sigmoid(x)


def _modulated(x, g, shift, scale):
    ms = jnp.mean(x * x, axis=-1, keepdims=True)
    return (x * lax.rsqrt(ms + EPS) * g) * (1.0 + scale) + shift


def _group_of_tile(i, tm):
    return jnp.maximum((i * tm - M_CTX) // LAT_T + 1, 0)


def _mod_spec(layer, tm):
    return pl.BlockSpec((None, None, 6, D), lambda i, *_: (layer, _group_of_tile(i, tm), 0, 0))


def _mod_body(c_ref, w_ref, b_ref, o_ref):
    s = _silu(c_ref[...])
    o_ref[...] = _dot(s.astype(BF), w_ref[...].astype(BF)) + b_ref[...]


def _modulation(cmat, ada_w, ada_b):
    tn = 1536
    return pl.pallas_call(
        _mod_body,
        grid=(DEPTH, 6 * D // tn),
        in_specs=[pl.BlockSpec((8, D), lambda l, j: (0, 0)),
                  pl.BlockSpec((None, D, tn), lambda l, j: (l, 0, j)),
                  pl.BlockSpec((None, 1, tn), lambda l, j: (l, 0, j))],
        out_specs=pl.BlockSpec((None, 8, tn), lambda l, j: (l, 0, j)),
        out_shape=jax.ShapeDtypeStruct((DEPTH, 8, 6 * D), F32),
        compiler_params=_cp("arbitrary", "arbitrary"),
        name="adaln_modulation",
    )(cmat, ada_w, ada_b.reshape(DEPTH, 1, 6 * D))


def _nl_body(*refs, srow, has_t):
    if has_t:
        x_ref, mod_ref, g_ref, w_ref, wt_ref, o_ref, ot_ref, h_scr = refs
    else:
        x_ref, mod_ref, g_ref, w_ref, o_ref, h_scr = refs

    @pl.when(pl.program_id(1) == 0)
    def _():
        h = _modulated(x_ref[...], g_ref[...], mod_ref[srow:srow + 1, :], mod_ref[srow + 1:srow + 2, :])
        hb = h.astype(BF)
        h_scr[...] = hb
        if has_t:
            ot_ref[...] = _dot_nt(wt_ref[...], hb)

    o_ref[...] = _dot(h_scr[...], w_ref[...]).astype(o_ref.dtype)


def _norm_linear(x, mod, layer, srow, gain, w, wt=None, *, tm=512, tn=1024, out_dtype=F32):
    m, n = x.shape[0], w.shape[1]
    has_t = wt is not None
    in_specs = [pl.BlockSpec((tm, D), lambda i, j: (i, 0)),
                _mod_spec(layer, tm),
                pl.BlockSpec((1, D), lambda i, j: (0, 0)),
                pl.BlockSpec((D, tn), lambda i, j: (0, j))]
    out_specs = [pl.BlockSpec((tm, tn), lambda i, j: (i, j))]
    out_shape = [jax.ShapeDtypeStruct((m, n), out_dtype)]
    args = [x, mod, gain.reshape(1, D), w]
    if has_t:
        r = wt.shape[0]
        in_specs.append(pl.BlockSpec((r, D), lambda i, j: (0, 0)))
        out_specs.append(pl.BlockSpec((r, tm), lambda i, j: (0, i)))
        out_shape.append(jax.ShapeDtypeStruct((r, m), F32))
        args.append(wt)
    res = pl.pallas_call(
        functools.partial(_nl_body, srow=srow, has_t=has_t),
        grid=(m // tm, n // tn),
        in_specs=in_specs, out_specs=out_specs, out_shape=out_shape,
        scratch_shapes=[pltpu.VMEM((tm, D), BF)],
        compiler_params=_cp("arbitrary", "arbitrary"),
        name="modulate_linear",
    )(*args)
    return res if has_t else res[0]


def _lr_body(x_ref, a_ref, w_ref, mod_ref, o_ref, *, grow):
    o_ref[...] = x_ref[...] + mod_ref[grow:grow + 1, :] * _dot(a_ref[...], w_ref[...])


def _linear_residual(x, a, w, mod, layer, grow, *, tm=512):
    m = x.shape[0]
    return pl.pallas_call(
        functools.partial(_lr_body, grow=grow),
        grid=(m // tm,),
        in_specs=[pl.BlockSpec((tm, D), lambda i: (i, 0)),
                  pl.BlockSpec((tm, D), lambda i: (i, 0)),
                  pl.BlockSpec((D, D), lambda i: (0, 0)),
                  _mod_spec(layer, tm)],
        out_specs=pl.BlockSpec((tm, D), lambda i: (i, 0)),
        out_shape=jax.ShapeDtypeStruct((m, D), F32),
        compiler_params=_cp("arbitrary"),
        name="linear_gated_residual",
    )(x, a, w, mod)


DN_C = 128


def _dn_body(*refs, T, has_s0):
    if has_s0:
        (pq_ref, pk_ref, pv_ref, pz_ref, cq_ref, ck_ref, cv_ref, gt_ref, par_ref, ng_ref, s0_ref,
         o_ref, sf_ref, of_scr) = refs
    else:
        (pq_ref, pk_ref, pv_ref, pz_ref, cq_ref, ck_ref, cv_ref, gt_ref, par_ref, ng_ref,
         o_ref, sf_ref, of_scr) = refs
    C = DN_C
    n = T // C
    row = lax.broadcasted_iota(jnp.int32, (T, 1), 0)

    def conv_silu(p_ref, c_ref):
        x = p_ref[...]
        w = c_ref[...]
        xp = jnp.where(row == 0, 0.0, pltpu.roll(x, 1, 0))
        xn = jnp.where(row == T - 1, 0.0, pltpu.roll(x, T - 1, 0))
        return _silu(xp * w[0:1, :] + x * w[1:2, :] + xn * w[2:3, :])

    def l2n(x):
        return x * lax.rsqrt(jnp.sum(x * x, axis=-1, keepdims=True) + EPS)

    q = l2n(conv_silu(pq_ref, cq_ref)) * (DN_DK ** -0.5)
    k = l2n(conv_silu(pk_ref, ck_ref))
    v = conv_silu(pv_ref, cv_ref)
    qb = q.astype(BF)
    kb = k.astype(BF)
    kT = k.T

    gt = gt_ref[...]
    par = par_ref[...]

    ii = lax.broadcasted_iota(jnp.int32, (C, C), 0)
    jj = lax.broadcasted_iota(jnp.int32, (C, C), 1)
    eye = ii == jj
    ones_c = jnp.ones((C, DN_DK), BF)

    for d in range(2):
        incl = (ii >= jj) if d == 0 else (ii <= jj)
        strict = (ii > jj) if d == 0 else (ii < jj)
        last = C - 1 if d == 0 else 0
        z = gt[2 * d:2 * d + 1, :] + par[2 * d + 1:2 * d + 2, 0:1]
        sp = jnp.maximum(z, 0.0) + jnp.log(1.0 + jnp.exp(-jnp.abs(z)))
        g_row = -jnp.exp(par[2 * d:2 * d + 1, 0:1]) * sp
        beta_row = _sigmoid(gt[2 * d + 1:2 * d + 2, :])
        rhs_c = jnp.concatenate([jnp.where(strict, 1.0, 0.0).astype(BF), ones_c], axis=1)
        merge = []
        sz = 1
        while sz < C:
            same = (ii // (2 * sz)) == (jj // (2 * sz))
            hi_i, hi_j = (ii % (2 * sz)) >= sz, (jj % (2 * sz)) >= sz
            merge.append(same & (hi_i & ~hi_j if d == 0 else ~hi_i & hi_j))
            sz *= 2

        s = s0_ref[d] if has_s0 else jnp.zeros((DN_DK, DN_DK), F32)
        order = range(n) if d == 0 else range(n - 1, -1, -1)
        for c in order:
            r0 = c * C
            q_c, k_c, v_c = q[r0:r0 + C], k[r0:r0 + C], v[r0:r0 + C]
            qb_c, kb_c = qb[r0:r0 + C], kb[r0:r0 + C]
            kk = _dot_nt(kb_c, kb_c)
            qk = _dot_nt(qb_c, kb_c)
            p_mat = jnp.where(incl, g_row[:, r0:r0 + C], 0.0)
            b_mat = jnp.where(eye, beta_row[:, r0:r0 + C], 0.0)
            lhs = jnp.concatenate([p_mat, b_mat], axis=0)
            hi, mid, lo = _split3(lhs)
            dg = _dot(hi, rhs_c) + _dot(mid, rhs_c) + _dot(lo, rhs_c)
            diff = dg[:C, :C]
            g_col = dg[:C, C:]
            b_col = dg[C:, C:]
            decay = jnp.where(incl, jnp.exp(jnp.where(incl, diff, 0.0)), 0.0)
            a_mat = jnp.where(strict, kk * decay * b_col, 0.0)
            inv = jnp.where(eye, 1.0, 0.0) - jnp.where(merge[0], a_mat, 0.0)
            for mk in merge[1:]:
                xb = inv.astype(BF)
                inv = inv - _dot(_dot(xb, jnp.where(mk, a_mat, 0.0).astype(BF)).astype(BF), xb)
            e_g = jnp.exp(g_col)
            rhs = jnp.concatenate([v_c * b_col, k_c * b_col * e_g], axis=1).astype(BF)
            sol = _dot(inv.astype(BF), rhs)
            u_c, w_c = sol[:, :DN_DK], sol[:, DN_DK:]
            qk = jnp.where(incl, qk * decay, 0.0)
            sb = s.astype(BF)
            v_new = u_c - _dot(w_c.astype(BF), sb)
            vnb = v_new.astype(BF)
            o_c = _dot((q_c * e_g).astype(BF), sb) + _dot(qk.astype(BF), vnb)
            kdec_t = kT[:, r0:r0 + C] * jnp.exp(diff[last:last + 1, :])
            s = s * jnp.exp(g_col[last:last + 1, :]) + _dot(kdec_t.astype(BF), vnb)
            if d == 0:
                of_scr[r0:r0 + C, :] = o_c
            else:
                of_scr[r0:r0 + C, :] = of_scr[r0:r0 + C, :] + o_c
        sf_ref[d] = s

    o = of_scr[...]
    ms = jnp.mean(o * o, axis=-1, keepdims=True)
    y = o * lax.rsqrt(ms + EPS) * ng_ref[...] * _silu(pz_ref[...])
    o_ref[...] = y.astype(o_ref.dtype)


def _deltanet(proj, gates_t, conv_w, par, norm_g, s0, *, T, nseq, row_block0):
    has_s0 = s0 is not None
    H = DN_H
    col = lambda off: pl.BlockSpec((T, 128), lambda b, h: (row_block0 + b, off + h))
    cw = lambda off: pl.BlockSpec((3, 128), lambda b, h: (0, off + h))
    in_specs = [col(0), col(H), col(2 * H), col(3 * H), cw(0), cw(H), cw(2 * H),
                pl.BlockSpec((8, T), lambda b, h: (h, row_block0 + b)),
                pl.BlockSpec((None, 8, 128), lambda b, h: (h, 0, 0)),
                pl.BlockSpec((1, 128), lambda b, h: (0, 0))]
    args = [proj, proj, proj, proj, conv_w, conv_w, conv_w, gates_t, par, norm_g.reshape(1, 128)]
    if has_s0:
        in_specs.append(pl.BlockSpec((None, 2, None, DN_DK, DN_DK), lambda b, h: (b, 0, h, 0, 0)))
        args.append(s0)
    return pl.pallas_call(
        functools.partial(_dn_body, T=T, has_s0=has_s0),
        grid=(nseq, H),
        in_specs=in_specs,
        out_specs=[pl.BlockSpec((T, 128), lambda b, h: (b, h)),
                   pl.BlockSpec((None, 2, None, DN_DK, DN_DK), lambda b, h: (b, 0, h, 0, 0))],
        out_shape=[jax.ShapeDtypeStruct((nseq * T, D), BF),
                   jax.ShapeDtypeStruct((nseq, 2, H, DN_DK, DN_DK), F32)],
        scratch_shapes=[pltpu.VMEM((T, DN_DK), F32)],
        compiler_params=_cp("arbitrary", "arbitrary"),
        name=f"deltanet_T{T}",
    )(*args)


def _dn_gate_weights(w_ab):
    w = w_ab.T.reshape(2, 2, DN_H, D).transpose(2, 0, 1, 3).reshape(DN_H, 4, D)
    return jnp.pad(w, ((0, 0), (0, 4), (0, 0))).reshape(DN_H * 8, D).astype(BF)


def _dn_gate_params(a_log, dt_bias):
    p = jnp.stack([a_log[0], dt_bias[0], a_log[1], dt_bias[1]], axis=1)
    p = jnp.pad(p, ((0, 0), (0, 4)))
    return jnp.broadcast_to(p[:, :, None], (DN_H, 8, 128)).astype(F32)


SC_CB = 256


def _sc_body(x_ref, mod_ref, g_ref, win_ref, cw_ref, wout_ref, o_ref):
    tm = x_ref.shape[0]
    x = x_ref[...]
    h = _modulated(x, g_ref[...], mod_ref[0:1, :], mod_ref[1:2, :]).astype(BF)
    period = jnp.where(pl.program_id(0) * tm < M_CTX, CTX_T, LAT_T)
    pos = lax.broadcasted_iota(jnp.int32, (tm, 1), 0) & (period - 1)
    first = pos == 0
    lastr = pos == period - 1
    acc = jnp.zeros((tm, D), F32)
    for j in range(D // SC_CB):
        c0 = j * SC_CB
        bg = _dot(h, win_ref[:, c0:c0 + SC_CB])
        cg = _dot(h, win_ref[:, D + c0:D + c0 + SC_CB])
        u = _dot(h, win_ref[:, 2 * D + c0:2 * D + c0 + SC_CB])
        t = cg * u
        tp = jnp.where(first, 0.0, pltpu.roll(t, 1, 0))
        tn = jnp.where(lastr, 0.0, pltpu.roll(t, tm - 1, 0))
        cw = cw_ref[:, c0:c0 + SC_CB]
        y = bg * (tp * cw[0:1, :] + t * cw[1:2, :] + tn * cw[2:3, :])
        acc = acc + _dot(y.astype(BF), wout_ref[c0:c0 + SC_CB, :])
    o_ref[...] = x + mod_ref[2:3, :] * acc


def _shortconv_layer(x, mod, layer, gain, w_in, conv_w, w_out, *, tm=1024):
    m = x.shape[0]
    return pl.pallas_call(
        _sc_body,
        grid=(m // tm,),
        in_specs=[pl.BlockSpec((tm, D), lambda i: (i, 0)),
                  _mod_spec(layer, tm),
                  pl.BlockSpec((1, D), lambda i: (0, 0)),
                  pl.BlockSpec((D, 3 * D), lambda i: (0, 0)),
                  pl.BlockSpec((3, D), lambda i: (0, 0)),
                  pl.BlockSpec((D, D), lambda i: (0, 0))],
        out_specs=pl.BlockSpec((tm, D), lambda i: (i, 0)),
        out_shape=jax.ShapeDtypeStruct((m, D), F32),
        compiler_params=_cp("arbitrary"),
        name="shortconv_mixer",
    )(x, mod, gain.reshape(1, D), w_in, conv_w, w_out)


NA_SCALE = NA_HD ** -0.5


def _rms64(x, g):
    return x * lax.rsqrt(jnp.mean(x * x, axis=-1, keepdims=True) + EPS) * g


def _na_ctx_body(q_ref, k_ref, v_ref, qn_ref, kn_ref, o_ref, ko_ref, vo_ref):
    for s in range(2):
        sl = slice(s * NA_HD, (s + 1) * NA_HD)
        qn = _rms64(q_ref[:, sl], qn_ref[...])
        kn = _rms64(k_ref[:, sl], kn_ref[...])
        v = v_ref[:, sl]
        ko_ref[s] = kn
        vo_ref[s] = v
        sc = _dot_nt(qn.astype(BF), kn.astype(BF)) * NA_SCALE
        p = jnp.exp(sc - jnp.max(sc, axis=-1, keepdims=True))
        o = _dot(p.astype(BF), v.astype(BF)) / jnp.sum(p, axis=-1, keepdims=True)
        o_ref[:, sl] = o.astype(o_ref.dtype)


def _na_context(qkv, q_norm, k_norm):
    hp = NA_H // 2
    T = CTX_T
    col = lambda off: pl.BlockSpec((T, 128), lambda b, h: (b, off + h))
    cache = pl.BlockSpec((None, None, 2, T, NA_HD), lambda b, h: (b, 0, h, 0, 0))
    nrm = pl.BlockSpec((1, NA_HD), lambda b, h: (0, 0))
    return pl.pallas_call(
        _na_ctx_body,
        grid=(N_CTX_SEQ, hp),
        in_specs=[col(0), col(hp), col(2 * hp), nrm, nrm],
        out_specs=[pl.BlockSpec((T, 128), lambda b, h: (b, h)), cache, cache],
        out_shape=[jax.ShapeDtypeStruct((M_CTX, D), BF),
                   jax.ShapeDtypeStruct((N_CTX_SEQ, 1, NA_H, T, NA_HD), F32),
                   jax.ShapeDtypeStruct((N_CTX_SEQ, 1, NA_H, T, NA_HD), F32)],
        compiler_params=_cp("arbitrary", "arbitrary"),
        name="na_context",
    )(qkv, qkv, qkv, q_norm.reshape(1, NA_HD), k_norm.reshape(1, NA_HD))


def _na_lat_body(q_ref, k_ref, v_ref, kc_ref, vc_ref, qn_ref, kn_ref, rpb_ref, o_ref, bias_scr):
    rows = LAT_T // GRID_W
    nloc = WIN_H * GRID_W
    qc = lax.broadcasted_iota(jnp.int32, (GRID_W, GRID_W), 0)
    kc = lax.broadcasted_iota(jnp.int32, (GRID_W, GRID_W), 1)
    cs = jnp.clip(qc - WIN_W // 2, 0, GRID_W - WIN_W)
    col_ok = (kc >= cs) & (kc < cs + WIN_W)
    n_dr = 2 * WIN_H - 1
    for s in range(2):
        sl = slice(s * NA_HD, (s + 1) * NA_HD)
        qn = (_rms64(q_ref[:, sl], qn_ref[...]) * NA_SCALE).astype(BF)
        kn = _rms64(k_ref[:, sl], kn_ref[...]).astype(BF)
        v = v_ref[:, sl].astype(BF)
        kctx = kc_ref[s].astype(BF)
        vctx = vc_ref[s].astype(BF)
        rp = rpb_ref[s]
        for dr in range(n_dr):
            rowv = jnp.broadcast_to(rp[dr:dr + 1, :], (GRID_W, 128))
            tz = pltpu.roll(rowv, 128 - (WIN_W - 1), 1, stride=1, stride_axis=0)
            bias_scr[:, dr * GRID_W:(dr + 1) * GRID_W] = jnp.where(col_ok, tz[:, :GRID_W], NEG_INF)
        s_ctx_all = _dot_nt(qn, kctx)
        for r in range(rows):
            kr0 = min(max(r - WIN_H // 2, 0), rows - WIN_H)
            off = kr0 - r + WIN_H - 1
            q_r = qn[r * GRID_W:(r + 1) * GRID_W]
            k_loc = kn[kr0 * GRID_W:kr0 * GRID_W + nloc]
            v_loc = v[kr0 * GRID_W:kr0 * GRID_W + nloc]
            s_loc = _dot_nt(q_r, k_loc) + bias_scr[:, off * GRID_W:off * GRID_W + nloc]
            s_ctx = s_ctx_all[r * GRID_W:(r + 1) * GRID_W]
            mx = jnp.maximum(jnp.max(s_loc, axis=-1, keepdims=True), jnp.max(s_ctx, axis=-1, keepdims=True))
            p_loc = jnp.exp(s_loc - mx)
            p_ctx = jnp.exp(s_ctx - mx)
            den = jnp.sum(p_loc, axis=-1, keepdims=True) + jnp.sum(p_ctx, axis=-1, keepdims=True)
            o = (_dot(p_loc.astype(BF), v_loc) + _dot(p_ctx.astype(BF), vctx)) / den
            o_ref[r * GRID_W:(r + 1) * GRID_W, sl] = o.astype(o_ref.dtype)


def _na_latent(qkv, cache_k, cache_v, j, q_norm, k_norm, rpb):
    hp = NA_H // 2
    T = LAT_T
    rb0 = M_CTX // T
    col = lambda off: pl.BlockSpec((T, 128), lambda b, h: (rb0 + b, off + h))
    cache = pl.BlockSpec((None, None, 2, PAST, NA_HD), lambda b, h: (b, j, h, 0, 0))
    nrm = pl.BlockSpec((1, NA_HD), lambda b, h: (0, 0))
    n_dr, n_dc = 2 * WIN_H - 1, 2 * WIN_W - 1
    return pl.pallas_call(
        _na_lat_body,
        grid=(N_LAT_SEQ, hp),
        in_specs=[col(0), col(hp), col(2 * hp), cache, cache, nrm, nrm,
                  pl.BlockSpec((None, 2, n_dr + 1, 128), lambda b, h: (j, h, 0, 0))],
        out_specs=pl.BlockSpec((T, 128), lambda b, h: (b, h)),
        out_shape=jax.ShapeDtypeStruct((M_LAT, D), BF),
        scratch_shapes=[pltpu.VMEM((GRID_W, n_dr * GRID_W), F32)],
        compiler_params=_cp("arbitrary", "arbitrary"),
        name="na_latent",
    )(qkv, qkv, qkv, cache_k, cache_v, q_norm.reshape(1, NA_HD), k_norm.reshape(1, NA_HD),
      jnp.pad(rpb, ((0, 0), (0, 0), (0, 1), (0, 128 - n_dc))))


def _ffn_body(x_ref, mod_ref, g_ref, wg_ref, wu_ref, wd_ref, o_ref, h_scr, acc_scr):
    f = pl.program_id(1)

    @pl.when(f == 0)
    def _():
        h = _modulated(x_ref[...], g_ref[...], mod_ref[3:4, :], mod_ref[4:5, :])
        h_scr[...] = h.astype(BF)
        acc_scr[...] = jnp.zeros_like(acc_scr)

    h = h_scr[...]
    a = _silu(_dot(h, wg_ref[...])) * _dot(h, wu_ref[...])
    acc_scr[...] += _dot(a.astype(BF), wd_ref[...])

    @pl.when(f == pl.num_programs(1) - 1)
    def _():
        o_ref[...] = x_ref[...] + mod_ref[5:6, :] * acc_scr[...]


def _dense_ffn_layer(x, mod, layer, gain, w_gu, w_down, *, tm=512, tf=1408):
    m = x.shape[0]
    nf = D_FF // tf
    return pl.pallas_call(
        _ffn_body,
        grid=(m // tm, nf),
        in_specs=[pl.BlockSpec((tm, D), lambda i, f: (i, 0)),
                  _mod_spec(layer, tm),
                  pl.BlockSpec((1, D), lambda i, f: (0, 0)),
                  pl.BlockSpec((D, tf), lambda i, f: (0, f)),
                  pl.BlockSpec((D, tf), lambda i, f: (0, nf + f)),
                  pl.BlockSpec((tf, D), lambda i, f: (f, 0))],
        out_specs=pl.BlockSpec((tm, D), lambda i, f: (i, 0)),
        out_shape=jax.ShapeDtypeStruct((m, D), F32),
        scratch_shapes=[pltpu.VMEM((tm, D), BF), pltpu.VMEM((tm, D), F32)],
        compiler_params=_cp("arbitrary", "arbitrary"),
        name="dense_swiglu",
    )(x, mod, gain.reshape(1, D), w_gu, w_gu, w_down)


def _moe_body(x_ref, mod_ref, g_ref, r_ref, wg_ref, wu_ref, wd_ref, o_ref, h_scr, acc_scr, comb_scr):
    e = pl.program_id(1)

    @pl.when(e == 0)
    def _():
        h = _modulated(x_ref[...], g_ref[...], mod_ref[3:4, :], mod_ref[4:5, :])
        h_scr[...] = h.astype(BF)
        acc_scr[...] = jnp.zeros_like(acc_scr)
        hh, hm, hl = _split3(h)
        rh, rm, rl = _split3(r_ref[...])
        lg = (_dot(hh, rh) + _dot(hh, rm) + _dot(hm, rh) + _dot(hh, rl) + _dot(hm, rm) + _dot(hl, rh))
        lane = lax.broadcasted_iota(jnp.int32, lg.shape, 1).astype(F32)
        m1 = jnp.max(lg, axis=-1, keepdims=True)
        i1 = jnp.min(jnp.where(lg == m1, lane, float(N_EXP)), axis=-1, keepdims=True)
        sel1 = lane == i1
        lg2 = jnp.where(sel1, -jnp.inf, lg)
        m2 = jnp.max(lg2, axis=-1, keepdims=True)
        i2 = jnp.min(jnp.where(lg2 == m2, lane, float(N_EXP)), axis=-1, keepdims=True)
        sel2 = lane == i2
        t = jnp.exp(m2 - m1)
        comb_scr[...] = jnp.where(sel1, 1.0 / (1.0 + t), 0.0) + jnp.where(sel2, t / (1.0 + t), 0.0)

    h = h_scr[...]
    lane = lax.broadcasted_iota(jnp.int32, comb_scr.shape, 1)
    c_e = jnp.sum(jnp.where(lane == e, comb_scr[...], 0.0), axis=-1, keepdims=True)
    a = _silu(_dot(h, wg_ref[...])) * _dot(h, wu_ref[...]) * c_e
    acc_scr[...] += _dot(a.astype(BF), wd_ref[...])

    @pl.when(e == pl.num_programs(1) - 1)
    def _():
        o_ref[...] = x_ref[...] + mod_ref[5:6, :] * acc_scr[...]


def _moe_layer(x, mod, layer, gain, router, w_gu, w_down, *, tm=1024):
    m = x.shape[0]
    return pl.pallas_call(
        _moe_body,
        grid=(m // tm, N_EXP),
        in_specs=[pl.BlockSpec((tm, D), lambda i, e: (i, 0)),
                  _mod_spec(layer, tm),
                  pl.BlockSpec((1, D), lambda i, e: (0, 0)),
                  pl.BlockSpec((D, N_EXP), lambda i, e: (0, 0)),
                  pl.BlockSpec((None, D, D_FFE), lambda i, e: (e, 0, 0)),
                  pl.BlockSpec((None, D, D_FFE), lambda i, e: (e, 0, 1)),
                  pl.BlockSpec((None, D_FFE, D), lambda i, e: (e, 0, 0))],
        out_specs=pl.BlockSpec((tm, D), lambda i, e: (i, 0)),
        out_shape=jax.ShapeDtypeStruct((m, D), F32),
        scratch_shapes=[pltpu.VMEM((tm, D), BF), pltpu.VMEM((tm, D), F32), pltpu.VMEM((tm, N_EXP), F32)],
        compiler_params=_cp("arbitrary", "arbitrary"),
        name="moe_swiglu",
    )(x, mod, gain.reshape(1, D), router, w_gu, w_gu, w_down)


def kernel(x_prompt, x_sample, state_dn, cache_na_k, cache_na_v, c, c_ctx, ada_w, ada_b, norm_mix, norm_ffn, dn_w_in, dn_w_ab, dn_conv, dn_a_log, dn_dt_bias, dn_norm, dn_w_out, sc_w_in, sc_conv, sc_w_out, na_w_qkv, na_q_norm, na_k_norm, na_rpb, na_w_out, ff_w_gu, ff_w_down, moe_router, moe_w_gu, moe_w_down):
    x = jnp.concatenate([x_prompt.reshape(M_CTX, D), x_sample.reshape(M_LAT, D)], axis=0)
    cmat = jnp.concatenate([c_ctx[None, :], c, jnp.zeros((3, D), F32)], axis=0)
    mod = _modulation(cmat, ada_w, ada_b).reshape(DEPTH, 8, 6, D)

    new_dn, new_k, new_v = [], [], []
    for i in range(DEPTH):
        kind, j = i % 3, i // 3
        if kind == 0:
            proj, gates_t = _norm_linear(x, mod, i, 0, norm_mix[i], dn_w_in[j].astype(BF),
                                         _dn_gate_weights(dn_w_ab[j]))
            par = _dn_gate_params(dn_a_log[j], dn_dt_bias[j])
            o_ctx, st = _deltanet(proj, gates_t, dn_conv[j], par, dn_norm[j], None,
                                  T=CTX_T, nseq=N_CTX_SEQ, row_block0=0)
            o_lat, _ = _deltanet(proj, gates_t, dn_conv[j], par, dn_norm[j], state_dn[:, j],
                                 T=LAT_T, nseq=N_LAT_SEQ, row_block0=M_CTX // LAT_T)
            new_dn.append(st)
            x = _linear_residual(x, jnp.concatenate([o_ctx, o_lat], axis=0), dn_w_out[j].astype(BF), mod, i, 2)
        elif kind == 1:
            x = _shortconv_layer(x, mod, i, norm_mix[i], sc_w_in[j].astype(BF), sc_conv[j],
                                 sc_w_out[j].astype(BF))
        else:
            qkv = _norm_linear(x, mod, i, 0, norm_mix[i], na_w_qkv[j].astype(BF))
            o_ctx, kc, vc = _na_context(qkv, na_q_norm[j], na_k_norm[j])
            o_lat = _na_latent(qkv, cache_na_k, cache_na_v, j, na_q_norm[j], na_k_norm[j], na_rpb)
            new_k.append(kc)
            new_v.append(vc)
            x = _linear_residual(x, jnp.concatenate([o_ctx, o_lat], axis=0), na_w_out[j].astype(BF), mod, i, 2)
        f = i // 2
        if i % 2 == 0:
            x = _dense_ffn_layer(x, mod, i, norm_ffn[i], ff_w_gu[f].astype(BF), ff_w_down[f].astype(BF))
        else:
            x = _moe_layer(x, mod, i, norm_ffn[i], moe_router[f], moe_w_gu[f].astype(BF),
                           moe_w_down[f].astype(BF))

    y_prompt = x[:M_CTX].reshape(N_CTX_SEQ, CTX_T, D)
    y_sample = x[M_CTX:].reshape(N_LAT_SEQ, LAT_T, D)
    return (y_prompt, y_sample, jnp.stack(new_dn, axis=1),
            jnp.concatenate(new_k, axis=1), jnp.concatenate(new_v, axis=1))
```
